```python
import jax, jax.numpy as jnp
from jax import lax
import numpy as np

D_MODEL = 1024
BATCH = 8
SEQ = 2048
DEPTH = 4

D_FF = 2816
CONV_WIDTH = 31
POOL_WINDOWS = (2, 4, 8, 16)
N_POOL_GROUPS = len(POOL_WINDOWS)
POOL_GROUP = D_MODEL // N_POOL_GROUPS
N_MIXERS = 2
N_CONV_LAYERS = (DEPTH + 1) // 2
N_POOL_LAYERS = DEPTH // 2
EPS = 1e-6

kernel_name = "hybrid_conv_pool_macaron_trunk"


def rms_norm(x, g):
    xf = x.astype(jnp.float32)
    y = xf * lax.rsqrt(jnp.mean(xf * xf, axis=-1, keepdims=True) + EPS)
    return (y * g.astype(jnp.float32)).astype(x.dtype)


def layer_norm(x, g, b):
    xf = x.astype(jnp.float32)
    mu = jnp.mean(xf, axis=-1, keepdims=True)
    xc = xf - mu
    var = jnp.mean(xc * xc, axis=-1, keepdims=True)
    y = xc * lax.rsqrt(var + EPS) * g.astype(jnp.float32) + b.astype(jnp.float32)
    return y.astype(x.dtype)


def swiglu(h, w_gate, w_up, w_down):
    return (jax.nn.silu(h @ w_gate) * (h @ w_up)) @ w_down


def conv_module(h, w_in, b_in, dw, dw_b, ln_g, ln_b, w_out, b_out):
    u = h @ w_in + b_in
    a, gate = jnp.split(u, 2, axis=-1)
    v = a * jax.nn.sigmoid(gate)
    rhs = dw[:, None, :].astype(v.dtype)
    v = lax.conv_general_dilated(
        v, rhs, window_strides=(1,), padding=[(CONV_WIDTH - 1, 0)],
        dimension_numbers=("NWC", "WIO", "NWC"), feature_group_count=D_MODEL)
    v = v + dw_b
    v = layer_norm(v, ln_g, ln_b)
    v = jax.nn.silu(v)
    return v @ w_out + b_out


def pool_mixer(h, w_groups, scale):
    T = h.shape[1]
    hf = h.astype(jnp.float32)
    csum = jnp.cumsum(hf, axis=1)
    t_idx = jnp.arange(T)
    outs = []
    for g, w in enumerate(POOL_WINDOWS):
        sl = slice(g * POOL_GROUP, (g + 1) * POOL_GROUP)
        c = csum[:, :, sl]
        lagged = jnp.pad(c[:, :T - w], ((0, 0), (w, 0), (0, 0)))
        cnt = jnp.minimum(t_idx + 1, w).astype(jnp.float32)[None, :, None]
        pooled = (c - lagged) / cnt - hf[:, :, sl]
        outs.append(jnp.einsum("btc,cd->btd", pooled.astype(h.dtype), w_groups[g]))
    return jnp.concatenate(outs, axis=-1) * scale


def setup_inputs(seed: int = 0) -> dict:
    key = jax.random.key(seed)
    ks = jax.random.split(key, 20)
    f32 = jnp.float32
    D, F, K, Dg = D_MODEL, D_FF, CONV_WIDTH, POOL_GROUP
    nrm = lambda k, shape, s: jax.random.normal(k, shape, f32) * s
    x = jax.random.normal(ks[0], (BATCH, SEQ, D), f32)
    ffn_norm = 1.0 + nrm(ks[1], (DEPTH, 2, D), 0.02)
    ffn_w_gate = nrm(ks[2], (DEPTH, 2, D, F), D ** -0.5)
    ffn_w_up = nrm(ks[3], (DEPTH, 2, D, F), D ** -0.5)
    ffn_w_down = nrm(ks[4], (DEPTH, 2, F, D), F ** -0.5)
    mix_norm = 1.0 + nrm(ks[5], (DEPTH, D), 0.02)
    conv_w_in = nrm(ks[6], (N_CONV_LAYERS, D, 2 * D), D ** -0.5)
    conv_b_in = nrm(ks[7], (N_CONV_LAYERS, 2 * D), 0.02)
    conv_dw = nrm(ks[8], (N_CONV_LAYERS, K, D), K ** -0.5)
    conv_dw_b = nrm(ks[9], (N_CONV_LAYERS, D), 0.02)
    conv_ln_g = 1.0 + nrm(ks[10], (N_CONV_LAYERS, D), 0.02)
    conv_ln_b = nrm(ks[11], (N_CONV_LAYERS, D), 0.02)
    conv_w_out = nrm(ks[12], (N_CONV_LAYERS, D, D), D ** -0.5)
    conv_b_out = nrm(ks[13], (N_CONV_LAYERS, D), 0.02)
    pool_w = nrm(ks[14], (N_POOL_LAYERS, N_POOL_GROUPS, Dg, Dg), Dg ** -0.5)
    pool_scale = 1.0 + nrm(ks[15], (N_POOL_LAYERS, D), 0.02)
    final_norm = 1.0 + nrm(ks[16], (D,), 0.02)
    return {"x": x, "ffn_norm": ffn_norm, "ffn_w_gate": ffn_w_gate, "ffn_w_up": ffn_w_up,
            "ffn_w_down": ffn_w_down, "mix_norm": mix_norm, "conv_w_in": conv_w_in,
            "conv_b_in": conv_b_in, "conv_dw": conv_dw, "conv_dw_b": conv_dw_b,
            "conv_ln_g": conv_ln_g, "conv_ln_b": conv_ln_b, "conv_w_out": conv_w_out,
            "conv_b_out": conv_b_out, "pool_w": pool_w, "pool_scale": pool_scale,
            "final_norm": final_norm}


def reference(x, ffn_norm, ffn_w_gate, ffn_w_up, ffn_w_down, mix_norm, conv_w_in,
              conv_b_in, conv_dw, conv_dw_b, conv_ln_g, conv_ln_b, conv_w_out,
              conv_b_out, pool_w, pool_scale, final_norm):
    for i in range(DEPTH):
        x = x + 0.5 * swiglu(rms_norm(x, ffn_norm[i, 0]), ffn_w_gate[i, 0],
                             ffn_w_up[i, 0], ffn_w_down[i, 0])
        h = rms_norm(x, mix_norm[i])
        j = i // N_MIXERS
        if i % N_MIXERS == 0:
            x = x + conv_module(h, conv_w_in[j], conv_b_in[j], conv_dw[j], conv_dw_b[j],
                                conv_ln_g[j], conv_ln_b[j], conv_w_out[j], conv_b_out[j])
        else:
            x = x + pool_mixer(h, pool_w[j], pool_scale[j])
        x = x + 0.5 * swiglu(rms_norm(x, ffn_norm[i, 1]), ffn_w_gate[i, 1],
                             ffn_w_up[i, 1], ffn_w_down[i, 1])
    return rms_norm(x, final_norm)
```

```python
import functools

import jax
import jax.numpy as jnp
from jax import lax
from jax.experimental import pallas as pl
from jax.experimental.pallas import tpu as pltpu

EPS = 1e-6
CONV_WIDTH = 31
POOL_WINDOWS = (2, 4, 8, 16)
N_MIXERS = 2

LANES = 128
SUBLANES = 8
MXU_DIM = 256
VMEM_LIMIT_BYTES = 56 * 1024 * 1024

FFN_ROWS = 512
FFN_CHUNK = 3 * MXU_DIM
MIX_ROWS = 256
CONV_HALO = 32
POOL_HALO = 16
CONV_ROW_BLOCK = 64

_BF16 = jnp.bfloat16
_F32 = jnp.float32


def _rms_norm(x, gain):
    ms = jnp.mean(x * x, axis=-1, keepdims=True)
    return x * lax.rsqrt(ms + EPS) * gain


def _resident(shape):
    zeros = (0,) * len(shape)
    return pl.BlockSpec(shape, lambda *_: zeros, pipeline_mode=pl.Buffered(1))


def _chunks(total, size):
    return [(s, min(size, total - s)) for s in range(0, total, size)]


def _ffn_kernel(x_ref, gain_ref, wg_ref, wu_ref, wd_ref, fgain_ref, o_ref, *, final_norm):
    x = x_ref[...]
    h = _rms_norm(x, gain_ref[...]).astype(_BF16)
    acc = None
    for start, width in _chunks(wg_ref.shape[1], FFN_CHUNK):
        g = jnp.dot(h, wg_ref[:, start:start + width], preferred_element_type=_F32)
        u = jnp.dot(h, wu_ref[:, start:start + width], preferred_element_type=_F32)
        a = (g * jax.nn.sigmoid(g) * u).astype(_BF16)
        d = jnp.dot(a, wd_ref[start:start + width, :], preferred_element_type=_F32)
        acc = d if acc is None else acc + d
    y = x + 0.5 * acc
    if final_norm:
        y = _rms_norm(y, fgain_ref[...])
    o_ref[...] = y


def _ffn(x2, gain, wg, wu, wd, fgain, final_norm):
    n, d = x2.shape
    f = wg.shape[1]
    row_spec = pl.BlockSpec((FFN_ROWS, d), lambda i: (i, 0))
    return pl.pallas_call(
        functools.partial(_ffn_kernel, final_norm=final_norm),
        grid=(n // FFN_ROWS,),
        in_specs=[row_spec, _resident((1, d)), _resident((d, f)), _resident((d, f)),
                  _resident((f, d)), _resident((1, d))],
        out_specs=row_spec,
        out_shape=jax.ShapeDtypeStruct((n, d), _F32),
        compiler_params=pltpu.CompilerParams(
            dimension_semantics=("arbitrary",), vmem_limit_bytes=VMEM_LIMIT_BYTES),
        name="ffn_final" if final_norm else "ffn",
    )(x2, gain, wg, wu, wd, fgain)


def _conv_kernel(x_ref, gain_ref, win_ref, bin_ref, dw_ref, dwb_ref, lng_ref, lnb_ref,
                 wout_ref, bout_ref, o_ref, vbuf_ref, y_ref):
    rows, d = x_ref.shape

    @pl.when(pl.program_id(1) == 0)
    def _():
        vbuf_ref[0:CONV_HALO, :] = jnp.zeros((CONV_HALO, d), _F32)

    x = x_ref[...]
    h = _rms_norm(x, gain_ref[...]).astype(_BF16)
    u = jnp.dot(h, win_ref[...], preferred_element_type=_F32) + bin_ref[...]
    vbuf_ref[CONV_HALO:CONV_HALO + rows, :] = u[:, :d] * jax.nn.sigmoid(u[:, d:])

    first = CONV_HALO - (CONV_WIDTH - 1)
    for r0 in range(0, rows, CONV_ROW_BLOCK):
        for l0 in range(0, d, LANES):
            acc = jnp.broadcast_to(dwb_ref[:, l0:l0 + LANES], (CONV_ROW_BLOCK, LANES))
            for k in range(CONV_WIDTH):
                tap = vbuf_ref[r0 + first + k:r0 + first + k + CONV_ROW_BLOCK, l0:l0 + LANES]
                acc = acc + dw_ref[k:k + 1, l0:l0 + LANES] * tap
            y_ref[r0:r0 + CONV_ROW_BLOCK, l0:l0 + LANES] = acc

    vbuf_ref[0:CONV_HALO, :] = vbuf_ref[rows:rows + CONV_HALO, :]

    y = y_ref[...]
    mu = jnp.mean(y, axis=-1, keepdims=True)
    yc = y - mu
    var = jnp.mean(yc * yc, axis=-1, keepdims=True)
    z = yc * lax.rsqrt(var + EPS) * lng_ref[...] + lnb_ref[...]
    z = (z * jax.nn.sigmoid(z)).astype(_BF16)
    o_ref[...] = x + jnp.dot(z, wout_ref[...], preferred_element_type=_F32) + bout_ref[...]


def _conv_module(x3, gain, w_in, b_in, dw, dw_b, ln_g, ln_b, w_out, b_out):
    b, t, d = x3.shape
    row_spec = pl.BlockSpec((None, MIX_ROWS, d), lambda i, j: (i, j, 0))
    return pl.pallas_call(
        _conv_kernel,
        grid=(b, t // MIX_ROWS),
        in_specs=[row_spec, _resident((1, d)), _resident((d, 2 * d)), _resident((1, 2 * d)),
                  _resident((CONV_WIDTH, d)), _resident((1, d)), _resident((1, d)),
                  _resident((1, d)), _resident((d, d)), _resident((1, d))],
        out_specs=row_spec,
        out_shape=jax.ShapeDtypeStruct((b, t, d), _F32),
        scratch_shapes=[pltpu.VMEM((CONV_HALO + MIX_ROWS, d), _F32),
                        pltpu.VMEM((MIX_ROWS, d), _F32)],
        compiler_params=pltpu.CompilerParams(
            dimension_semantics=("arbitrary", "arbitrary"), vmem_limit_bytes=VMEM_LIMIT_BYTES),
        name="conv_module",
    )(x3, gain, w_in, b_in, dw, dw_b, ln_g, ln_b, w_out, b_out)


def _pool_kernel(x_ref, gain_ref, w_ref, scale_ref, o_ref, hbuf_ref, sbuf_ref):
    rows, d = x_ref.shape
    group = d // len(POOL_WINDOWS)
    j = pl.program_id(1)

    @pl.when(j == 0)
    def _():
        hbuf_ref[0:POOL_HALO, :] = jnp.zeros((POOL_HALO, d), _F32)

    x = x_ref[...]
    h = _rms_norm(x, gain_ref[...])
    hbuf_ref[POOL_HALO:POOL_HALO + rows, :] = h

    t_idx = j * rows + lax.broadcasted_iota(jnp.int32, (rows, 1), 0)
    outs = []
    for g, window in enumerate(POOL_WINDOWS):
        lanes = slice(g * group, (g + 1) * group)
        sbuf_ref[:, lanes] = hbuf_ref[:, lanes]
        span = 1
        while span < window:
            lo = POOL_HALO - (window - 2 * span)
            lo = max(lo - (lo % SUBLANES), span)
            cur = sbuf_ref[lo:POOL_HALO + rows, lanes]
            lag = sbuf_ref[lo - span:POOL_HALO + rows - span, lanes]
            sbuf_ref[lo:POOL_HALO + rows, lanes] = cur + lag
            span *= 2
        total = sbuf_ref[POOL_HALO:POOL_HALO + rows, lanes]
        cnt = jnp.minimum(t_idx + 1, window).astype(_F32)
        pooled = (total / cnt - h[:, lanes]).astype(_BF16)
        outs.append(jnp.dot(pooled, w_ref[g], preferred_element_type=_F32))
    hbuf_ref[0:POOL_HALO, :] = hbuf_ref[rows:rows + POOL_HALO, :]
    o_ref[...] = x + jnp.concatenate(outs, axis=-1) * scale_ref[...]


def _pool_mixer(x3, gain, w_groups, scale):
    b, t, d = x3.shape
    row_spec = pl.BlockSpec((None, MIX_ROWS, d), lambda i, j: (i, j, 0))
    return pl.pallas_call(
        _pool_kernel,
        grid=(b, t // MIX_ROWS),
        in_specs=[row_spec, _resident((1, d)), _resident(w_groups.shape), _resident((1, d))],
        out_specs=row_spec,
        out_shape=jax.ShapeDtypeStruct((b, t, d), _F32),
        scratch_shapes=[pltpu.VMEM((POOL_HALO + MIX_ROWS, d), _F32),
                        pltpu.VMEM((POOL_HALO + MIX_ROWS, d), _F32)],
        compiler_params=pltpu.CompilerParams(
            dimension_semantics=("arbitrary", "arbitrary"), vmem_limit_bytes=VMEM_LIMIT_BYTES),
        name="pool_mixer",
    )(x3, gain, w_groups, scale)


def kernel(x, ffn_norm, ffn_w_gate, ffn_w_up, ffn_w_down, mix_norm, conv_w_in, conv_b_in,
           conv_dw, conv_dw_b, conv_ln_g, conv_ln_b, conv_w_out, conv_b_out, pool_w,
           pool_scale, final_norm):
    b, t, d = x.shape
    depth = ffn_norm.shape[0]
    assert t % MIX_ROWS == 0 and (b * t) % FFN_ROWS == 0 and d % LANES == 0
    row = lambda v: v.reshape(1, -1)
    wg, wu, wd = (w.astype(_BF16) for w in (ffn_w_gate, ffn_w_up, ffn_w_down))
    w_in, w_out, w_pool = (w.astype(_BF16) for w in (conv_w_in, conv_w_out, pool_w))
    fgain = row(final_norm)

    def ffn(x3, i, half, last):
        y = _ffn(x3.reshape(b * t, d), row(ffn_norm[i, half]), wg[i, half], wu[i, half],
                 wd[i, half], fgain, last)
        return y.reshape(b, t, d)

    for i in range(depth):
        x = ffn(x, i, 0, False)
        j = i // N_MIXERS
        if i % N_MIXERS == 0:
            x = _conv_module(x, row(mix_norm[i]), w_in[j], row(conv_b_in[j]), conv_dw[j],
                             row(conv_dw_b[j]), row(conv_ln_g[j]), row(conv_ln_b[j]),
                             w_out[j], row(conv_b_out[j]))
        else:
            x = _pool_mixer(x, row(mix_norm[i]), w_pool[j], row(pool_scale[j]))
        x = ffn(x, i, 1, i == depth - 1)
    return x
```

```python
import functools

import jax
import jax.numpy as jnp
from jax import lax
from jax.experimental import pallas as pl
from jax.experimental.pallas import tpu as pltpu

EPS = 1e-6
CONV_WIDTH = 31
POOL_WINDOWS = (2, 4, 8, 16)
N_MIXERS = 2

LANES = 128
SUBLANES = 8
MXU_DIM = 256
VMEM_LIMIT_BYTES = 56 * 1024 * 1024

FFN_ROWS = 512
FFN_CHUNK = 3 * MXU_DIM
MIX_ROWS = 256
CONV_HALO = 32
POOL_HALO = 16
CONV_ROW_BLOCK = 64

_F32 = jnp.float32


def _rms_norm(x, gain):
    ms = jnp.mean(x * x, axis=-1, keepdims=True)
    return x * lax.rsqrt(ms + EPS) * gain


def _param(stacked, index):
    lead = len(index)
    tail = stacked.shape[lead:]
    at = tuple(index) + (0,) * len(tail)
    return pl.BlockSpec((None,) * lead + tail, lambda *_: at, pipeline_mode=pl.Buffered(1))


def _rows(stacked):
    return stacked.reshape(stacked.shape[:-1] + (1, stacked.shape[-1]))


def _chunks(total, size):
    return [(s, min(size, total - s)) for s in range(0, total, size)]


def _ffn_kernel(x_ref, gain_ref, wg_ref, wu_ref, wd_ref, fgain_ref, o_ref, *, final_norm):
    x = x_ref[...]
    h = _rms_norm(x, gain_ref[...])
    acc = None
    for start, width in _chunks(wg_ref.shape[1], FFN_CHUNK):
        g = jnp.dot(h, wg_ref[:, start:start + width], preferred_element_type=_F32)
        u = jnp.dot(h, wu_ref[:, start:start + width], preferred_element_type=_F32)
        a = g * jax.nn.sigmoid(g) * u
        d = jnp.dot(a, wd_ref[start:start + width, :], preferred_element_type=_F32)
        acc = d if acc is None else acc + d
    y = x + 0.5 * acc
    if final_norm:
        y = _rms_norm(y, fgain_ref[...])
    o_ref[...] = y


def _ffn(x2, gains, wg, wu, wd, fgain, index, final_norm):
    n, d = x2.shape
    row_spec = pl.BlockSpec((FFN_ROWS, d), lambda i: (i, 0))
    return pl.pallas_call(
        functools.partial(_ffn_kernel, final_norm=final_norm),
        grid=(n // FFN_ROWS,),
        in_specs=[row_spec, _param(gains, index), _param(wg, index), _param(wu, index),
                  _param(wd, index), _param(fgain, ())],
        out_specs=row_spec,
        out_shape=jax.ShapeDtypeStruct((n, d), _F32),
        compiler_params=pltpu.CompilerParams(
            dimension_semantics=("arbitrary",), vmem_limit_bytes=VMEM_LIMIT_BYTES),
        name="ffn_final" if final_norm else "ffn",
    )(x2, gains, wg, wu, wd, fgain)


def _conv_kernel(x_ref, gain_ref, win_ref, bin_ref, dw_ref, dwb_ref, lng_ref, lnb_ref,
                 wout_ref, bout_ref, o_ref, vbuf_ref, y_ref):
    rows, d = x_ref.shape

    @pl.when(pl.program_id(1) == 0)
    def _():
        vbuf_ref[0:CONV_HALO, :] = jnp.zeros((CONV_HALO, d), _F32)

    x = x_ref[...]
    h = _rms_norm(x, gain_ref[...])
    u = jnp.dot(h, win_ref[...], preferred_element_type=_F32) + bin_ref[...]
    vbuf_ref[CONV_HALO:CONV_HALO + rows, :] = u[:, :d] * jax.nn.sigmoid(u[:, d:])

    first = CONV_HALO - (CONV_WIDTH - 1)
    window = CONV_ROW_BLOCK + CONV_HALO
    for r0 in range(0, rows, CONV_ROW_BLOCK):
        for l0 in range(0, d, LANES):
            src = vbuf_ref[r0:r0 + window, l0:l0 + LANES]
            acc = jnp.broadcast_to(dwb_ref[:, l0:l0 + LANES], (CONV_ROW_BLOCK, LANES))
            for phase in range(SUBLANES):
                shifted = src if phase == 0 else pltpu.roll(src, window - phase, axis=0)
                for k in range(CONV_WIDTH):
                    if (first + k) % SUBLANES != phase:
                        continue
                    q = first + k - phase
                    acc = acc + dw_ref[k:k + 1, l0:l0 + LANES] * shifted[q:q + CONV_ROW_BLOCK, :]
            y_ref[r0:r0 + CONV_ROW_BLOCK, l0:l0 + LANES] = acc

    vbuf_ref[0:CONV_HALO, :] = vbuf_ref[rows:rows + CONV_HALO, :]

    y = y_ref[...]
    mu = jnp.mean(y, axis=-1, keepdims=True)
    yc = y - mu
    var = jnp.mean(yc * yc, axis=-1, keepdims=True)
    z = yc * lax.rsqrt(var + EPS) * lng_ref[...] + lnb_ref[...]
    z = z * jax.nn.sigmoid(z)
    o_ref[...] = x + jnp.dot(z, wout_ref[...], preferred_element_type=_F32) + bout_ref[...]


def _conv_module(x3, gains, layer, w_in, b_in, dw, dw_b, ln_g, ln_b, w_out, b_out, j):
    b, t, d = x3.shape
    row_spec = pl.BlockSpec((None, MIX_ROWS, d), lambda i, s: (i, s, 0))
    params = (w_in, b_in, dw, dw_b, ln_g, ln_b, w_out, b_out)
    return pl.pallas_call(
        _conv_kernel,
        grid=(b, t // MIX_ROWS),
        in_specs=[row_spec, _param(gains, (layer,))] + [_param(p, (j,)) for p in params],
        out_specs=row_spec,
        out_shape=jax.ShapeDtypeStruct((b, t, d), _F32),
        scratch_shapes=[pltpu.VMEM((CONV_HALO + MIX_ROWS, d), _F32),
                        pltpu.VMEM((MIX_ROWS, d), _F32)],
        compiler_params=pltpu.CompilerParams(
            dimension_semantics=("arbitrary", "arbitrary"), vmem_limit_bytes=VMEM_LIMIT_BYTES),
        name="conv_module",
    )(x3, gains, *params)


def _pool_kernel(x_ref, gain_ref, w_ref, scale_ref, o_ref, hbuf_ref, sbuf_ref):
    rows, d = x_ref.shape
    group = d // len(POOL_WINDOWS)
    j = pl.program_id(1)

    @pl.when(j == 0)
    def _():
        hbuf_ref[0:POOL_HALO, :] = jnp.zeros((POOL_HALO, d), _F32)

    x = x_ref[...]
    h = _rms_norm(x, gain_ref[...])
    hbuf_ref[POOL_HALO:POOL_HALO + rows, :] = h

    t_idx = j * rows + lax.broadcasted_iota(jnp.int32, (rows, 1), 0)
    outs = []
    for g, window in enumerate(POOL_WINDOWS):
        lanes = slice(g * group, (g + 1) * group)
        sbuf_ref[:, lanes] = hbuf_ref[:, lanes]
        span = 1
        while span < window:
            lo = POOL_HALO - (window - 2 * span)
            lo = max(lo - (lo % SUBLANES), span)
            cur = sbuf_ref[lo:POOL_HALO + rows, lanes]
            lag = sbuf_ref[lo - span:POOL_HALO + rows - span, lanes]
            sbuf_ref[lo:POOL_HALO + rows, lanes] = cur + lag
            span *= 2
        total = sbuf_ref[POOL_HALO:POOL_HALO + rows, lanes]
        cnt = jnp.minimum(t_idx + 1, window).astype(_F32)
        pooled = total / cnt - h[:, lanes]
        outs.append(jnp.dot(pooled, w_ref[g], preferred_element_type=_F32))
    hbuf_ref[0:POOL_HALO, :] = hbuf_ref[rows:rows + POOL_HALO, :]
    o_ref[...] = x + jnp.concatenate(outs, axis=-1) * scale_ref[...]


def _pool_mixer(x3, gains, layer, w_groups, scale, j):
    b, t, d = x3.shape
    row_spec = pl.BlockSpec((None, MIX_ROWS, d), lambda i, s: (i, s, 0))
    return pl.pallas_call(
        _pool_kernel,
        grid=(b, t // MIX_ROWS),
        in_specs=[row_spec, _param(gains, (layer,)), _param(w_groups, (j,)), _param(scale, (j,))],
        out_specs=row_spec,
        out_shape=jax.ShapeDtypeStruct((b, t, d), _F32),
        scratch_shapes=[pltpu.VMEM((POOL_HALO + MIX_ROWS, d), _F32),
                        pltpu.VMEM((POOL_HALO + MIX_ROWS, d), _F32)],
        compiler_params=pltpu.CompilerParams(
            dimension_semantics=("arbitrary", "arbitrary"), vmem_limit_bytes=VMEM_LIMIT_BYTES),
        name="pool_mixer",
    )(x3, gains, w_groups, scale)


def kernel(x, ffn_norm, ffn_w_gate, ffn_w_up, ffn_w_down, mix_norm, conv_w_in, conv_b_in,
           conv_dw, conv_dw_b, conv_ln_g, conv_ln_b, conv_w_out, conv_b_out, pool_w,
           pool_scale, final_norm):
    b, t, d = x.shape
    depth = ffn_norm.shape[0]
    assert t % MIX_ROWS == 0 and (b * t) % FFN_ROWS == 0 and d % LANES == 0
    ffn_gains, mix_gains, fgain = _rows(ffn_norm), _rows(mix_norm), _rows(final_norm)
    conv_params = (conv_w_in, _rows(conv_b_in), conv_dw, _rows(conv_dw_b), _rows(conv_ln_g),
                   _rows(conv_ln_b), conv_w_out, _rows(conv_b_out))

    def ffn(x3, i, half, last):
        y = _ffn(x3.reshape(b * t, d), ffn_gains, ffn_w_gate, ffn_w_up, ffn_w_down, fgain,
                 (i, half), last)
        return y.reshape(b, t, d)

    for i in range(depth):
        x = ffn(x, i, 0, False)
        j = i // N_MIXERS
        if i % N_MIXERS == 0:
            x = _conv_module(x, mix_gains, i, *conv_params, j)
        else:
            x = _pool_mixer(x, mix_gains, i, pool_w, _rows(pool_scale), j)
        x = ffn(x, i, 1, i == depth - 1)
    return x
```

```python
import functools

import jax
import jax.numpy as jnp
from jax import lax
from jax.experimental import pallas as pl
from jax.experimental.pallas import tpu as pltpu

EPS = 1e-6
CONV_WIDTH = 31
POOL_WINDOWS = (2, 4, 8, 16)
N_MIXERS = 2

LANES = 128
SUBLANES = 8
MXU_DIM = 256
VMEM_LIMIT_BYTES = 56 * 1024 * 1024

TILE_ROWS = 512
FFN_CHUNK = MXU_DIM
FRONT_ROW_BLOCK = 128
STAGE_ROWS_WIDE = 64
STAGE_ROWS_D = 128
CONV_SUB_ROWS = 256
CONV_HALO = 32
POOL_HALO = 16
CONV_ROW_BLOCK = 64

_F32 = jnp.float32
_BF16 = jnp.bfloat16


def _rms_norm(x, gain):
    ms = jnp.mean(x * x, axis=-1, keepdims=True)
    return x * lax.rsqrt(ms + EPS) * gain


def _param(stacked, index):
    lead = len(index)
    tail = stacked.shape[lead:]
    at = tuple(index) + (0,) * len(tail)
    return pl.BlockSpec((None,) * lead + tail, lambda *_: at, pipeline_mode=pl.Buffered(1))


def _rows(stacked):
    return stacked.reshape(stacked.shape[:-1] + (1, stacked.shape[-1]))


def _chunks(total, size):
    return [(s, min(size, total - s)) for s in range(0, total, size)]


def _zero_at_sequence_start(halo, seq_pos):
    return jnp.where(jnp.full(halo.shape, seq_pos, jnp.int32) == 0, 0.0, halo)


def _stage_weights(tasks):
    copies, used = [], {}
    for src, dst, stage, sems in tasks:
        rows = stage.shape[1]
        for c in range(src.shape[0] // rows):
            slot = used.get(id(stage), 0) % 2
            used[id(stage)] = used.get(id(stage), 0) + 1
            copies.append((src.at[pl.ds(c * rows, rows), :], dst.at[pl.ds(c * rows, rows), :],
                           stage.at[slot], sems.at[slot]))

    def dma(k):
        src, _, slot_ref, sem = copies[k]
        return pltpu.make_async_copy(src, slot_ref, sem)

    dma(0).start()
    for k, (_, dst, slot_ref, _) in enumerate(copies):
        if k + 1 < len(copies):
            dma(k + 1).start()
        dma(k).wait()
        dst[...] = slot_ref[...].astype(_BF16)


def _swiglu_pieces(xbuf, hbuf, wg_ref, wu_ref, wd_ref, fgain_ref, o_ref, final_norm):
    state = {"acc": None}

    def chunk(start, width):
        def run():
            h = hbuf[...]
            g = jnp.dot(h, wg_ref[:, start:start + width], preferred_element_type=_F32)
            u = jnp.dot(h, wu_ref[:, start:start + width], preferred_element_type=_F32)
            a = (g * jax.nn.sigmoid(g) * u).astype(_BF16)
            d = jnp.dot(a, wd_ref[start:start + width, :], preferred_element_type=_F32)
            state["acc"] = d if state["acc"] is None else state["acc"] + d
        return run

    def finish():
        y = xbuf[...] + 0.5 * state["acc"]
        if final_norm:
            y = _rms_norm(y, fgain_ref[...])
        o_ref[...] = y

    return [chunk(s, w) for s, w in _chunks(wg_ref.shape[1], FFN_CHUNK)], finish


def _front_plain_pieces(x_ref, gain_ref, xbuf, hbuf):
    def block(r0):
        def run():
            x = x_ref[r0:r0 + FRONT_ROW_BLOCK, :]
            xbuf[r0:r0 + FRONT_ROW_BLOCK, :] = x
            hbuf[r0:r0 + FRONT_ROW_BLOCK, :] = _rms_norm(x, gain_ref[...]).astype(_BF16)
        return run
    return [("vector", block(r0)) for r0 in range(0, x_ref.shape[0], FRONT_ROW_BLOCK)]


def _front_conv_pieces(x_ref, gain_ref, xbuf, hbuf, seq_pos, mgain_ref, win_ref, bin_ref,
                       dw_ref, dwb_ref, lng_ref, lnb_ref, wout_ref, bout_ref, vbuf_ref, y_ref):
    rows, d = x_ref.shape
    sub = CONV_SUB_ROWS
    first = CONV_HALO - (CONV_WIDTH - 1)
    window = CONV_ROW_BLOCK + CONV_HALO
    pieces = []

    def head(p0):
        def run():
            if p0 == 0:
                vbuf_ref[0:CONV_HALO, :] = _zero_at_sequence_start(vbuf_ref[0:CONV_HALO, :],
                                                                   seq_pos)
            h = _rms_norm(x_ref[p0:p0 + sub, :], mgain_ref[...]).astype(_BF16)
            u = jnp.dot(h, win_ref[...], preferred_element_type=_F32) + bin_ref[...]
            vbuf_ref[CONV_HALO:CONV_HALO + sub, :] = u[:, :d] * jax.nn.sigmoid(u[:, d:])
        return run

    def taps(r0, l0):
        def run():
            src = vbuf_ref[r0:r0 + window, l0:l0 + LANES]
            acc = jnp.broadcast_to(dwb_ref[:, l0:l0 + LANES], (CONV_ROW_BLOCK, LANES))
            for phase in range(SUBLANES):
                shifted = src if phase == 0 else pltpu.roll(src, window - phase, axis=0)
                for k in range(CONV_WIDTH):
                    if (first + k) % SUBLANES != phase:
                        continue
                    q = first + k - phase
                    acc = acc + dw_ref[k:k + 1, l0:l0 + LANES] * shifted[q:q + CONV_ROW_BLOCK, :]
            y_ref[r0:r0 + CONV_ROW_BLOCK, l0:l0 + LANES] = acc
        return run

    def tail(p0):
        def run():
            vbuf_ref[0:CONV_HALO, :] = vbuf_ref[sub:sub + CONV_HALO, :]
            y = y_ref[...]
            mu = jnp.mean(y, axis=-1, keepdims=True)
            yc = y - mu
            var = jnp.mean(yc * yc, axis=-1, keepdims=True)
            z = yc * lax.rsqrt(var + EPS) * lng_ref[...] + lnb_ref[...]
            z = (z * jax.nn.sigmoid(z)).astype(_BF16)
            xn = (x_ref[p0:p0 + sub, :] + jnp.dot(z, wout_ref[...], preferred_element_type=_F32)
                  + bout_ref[...])
            xbuf[p0:p0 + sub, :] = xn
            hbuf[p0:p0 + sub, :] = _rms_norm(xn, gain_ref[...]).astype(_BF16)
        return run

    for p0 in range(0, rows, sub):
        pieces.append(("matmul", head(p0)))
        for r0 in range(0, sub, CONV_ROW_BLOCK):
            for l0 in range(0, d, LANES):
                pieces.append(("vector", taps(r0, l0)))
        pieces.append(("matmul", tail(p0)))
    return pieces


def _front_pool_pieces(x_ref, gain_ref, xbuf, hbuf, seq_pos, mgain_ref, w_ref, scale_ref,
                       hb_ref, sb_ref, out_ref):
    rows, d = x_ref.shape
    group = d // len(POOL_WINDOWS)

    def head():
        hb_ref[0:POOL_HALO, :] = _zero_at_sequence_start(hb_ref[0:POOL_HALO, :], seq_pos)
        hb_ref[POOL_HALO:POOL_HALO + rows, :] = _rms_norm(x_ref[...], mgain_ref[...])

    def pool_group(g, window):
        def run():
            lanes = slice(g * group, (g + 1) * group)
            sb_ref[:, lanes] = hb_ref[:, lanes]
            span = 1
            while span < window:
                lo = POOL_HALO - (window - 2 * span)
                lo = max(lo - (lo % SUBLANES), span)
                cur = sb_ref[lo:POOL_HALO + rows, lanes]
                lag = sb_ref[lo - span:POOL_HALO + rows - span, lanes]
                sb_ref[lo:POOL_HALO + rows, lanes] = cur + lag
                span *= 2
            total = sb_ref[POOL_HALO:POOL_HALO + rows, lanes]
            t_idx = seq_pos * rows + lax.broadcasted_iota(jnp.int32, (rows, 1), 0)
            cnt = jnp.minimum(t_idx + 1, window).astype(_F32)
            pooled = (total / cnt - hb_ref[POOL_HALO:POOL_HALO + rows, lanes]).astype(_BF16)
            out_ref[:, lanes] = jnp.dot(pooled, w_ref[g], preferred_element_type=_F32)
        return run

    def tail():
        hb_ref[0:POOL_HALO, :] = hb_ref[rows:rows + POOL_HALO, :]
        xn = x_ref[...] + out_ref[...] * scale_ref[...]
        xbuf[...] = xn
        hbuf[...] = _rms_norm(xn, gain_ref[...]).astype(_BF16)

    return ([("vector", head)]
            + [("vector", pool_group(g, w)) for g, w in enumerate(POOL_WINDOWS)]
            + [("vector", tail)])


def _emit_interleaved(front, chunks, finish):
    n_vector = sum(kind == "vector" for kind, _ in front)
    per_chunk = max(1, -(-n_vector // len(chunks)))
    chunks = list(chunks)
    seen = 0
    for kind, run in front:
        if kind == "vector":
            if seen % per_chunk == 0 and chunks:
                chunks.pop(0)()
            seen += 1
        run()
    for run in chunks:
        run()
    finish()


def _half_step_kernel(*refs, mixer, n_tiles, tiles_per_seq, index, mix_index, final_norm):
    n_mix_in = {"none": 0, "conv": 9, "pool": 3}[mixer]
    it = iter(refs)
    take = lambda k: [next(it) for _ in range(k)]
    x_ref, gain_ref, fgain_ref = take(3)
    wg_hbm, wu_hbm, wd_hbm = take(3)
    mix_in = take(n_mix_in)
    (o_ref,) = take(1)
    wg_ref, wu_ref, wd_ref, stage_f, stage_d, sems_f, sems_d = take(7)
    xbufs, hbufs = take(2), take(2)
    mix_scratch = list(it)

    s = pl.program_id(0)
    seq_pos = s % tiles_per_seq

    def front(slot):
        xbuf, hbuf = xbufs[slot], hbufs[slot]
        if mixer == "none":
            return _front_plain_pieces(x_ref, gain_ref, xbuf, hbuf)
        if mixer == "conv":
            mgain, _, b_in, dw, dwb, lng, lnb, _, b_out = mix_in
            win_ref, wout_ref, _, _, vbuf_ref, y_ref = mix_scratch
            return _front_conv_pieces(x_ref, gain_ref, xbuf, hbuf, seq_pos, mgain, win_ref, b_in,
                                      dw, dwb, lng, lnb, wout_ref, b_out, vbuf_ref, y_ref)
        mgain, _, scale = mix_in
        w_ref, _, _, hb_ref, sb_ref, out_ref = mix_scratch
        return _front_pool_pieces(x_ref, gain_ref, xbuf, hbuf, seq_pos, mgain, w_ref, scale,
                                  hb_ref, sb_ref, out_ref)

    def swiglu(slot):
        return _swiglu_pieces(xbufs[slot], hbufs[slot], wg_ref, wu_ref, wd_ref, fgain_ref, o_ref,
                              final_norm)

    @pl.when(s == 0)
    def _():
        tasks = [(wg_hbm.at[index], wg_ref, stage_f, sems_f),
                 (wu_hbm.at[index], wu_ref, stage_f, sems_f),
                 (wd_hbm.at[index], wd_ref, stage_d, sems_d)]
        if mixer == "conv":
            win_ref, wout_ref, stage_in, sems_in = mix_scratch[:4]
            tasks += [(mix_in[1].at[mix_index], win_ref, stage_in, sems_in),
                      (mix_in[7].at[mix_index], wout_ref, stage_d, sems_d)]
        elif mixer == "pool":
            w_ref, stage_p, sems_p = mix_scratch[:3]
            w_hbm = mix_in[1]
            tasks += [(w_hbm.at[mix_index, g], w_ref.at[g], stage_p, sems_p)
                      for g in range(w_ref.shape[0])]
        _stage_weights(tasks)
        if mixer == "conv":
            vbuf_ref = mix_scratch[4]
            vbuf_ref[0:CONV_HALO, :] = jnp.zeros((CONV_HALO, vbuf_ref.shape[1]), _F32)
        elif mixer == "pool":
            hb_ref = mix_scratch[3]
            hb_ref[0:POOL_HALO, :] = jnp.zeros((POOL_HALO, hb_ref.shape[1]), _F32)
        for _, run in front(0):
            run()

    for parity in (0, 1):
        @pl.when((s > 0) & (s < n_tiles) & (s % 2 == parity))
        def _():
            _emit_interleaved(front(parity), *swiglu(1 - parity))

    @pl.when(s == n_tiles)
    def _():
        _emit_interleaved([], *swiglu((n_tiles - 1) % 2))


def _half_step(x2, tiles_per_seq, gains, index, wg, wu, wd, fgain, final_norm,
               mixer="none", mix_index=None, mix_params=()):
    n, d = x2.shape
    f = wg.shape[-1]
    n_tiles = n // TILE_ROWS
    hbm = pl.BlockSpec(memory_space=pl.ANY)
    in_spec = pl.BlockSpec((TILE_ROWS, d), lambda s: (jnp.minimum(s, n_tiles - 1), 0))
    out_spec = pl.BlockSpec((TILE_ROWS, d), lambda s: (jnp.maximum(s - 1, 0), 0))

    scratch = [pltpu.VMEM((d, f), _BF16), pltpu.VMEM((d, f), _BF16), pltpu.VMEM((f, d), _BF16),
               pltpu.VMEM((2, STAGE_ROWS_WIDE, f), _F32), pltpu.VMEM((2, STAGE_ROWS_D, d), _F32),
               pltpu.SemaphoreType.DMA((2,)), pltpu.SemaphoreType.DMA((2,)),
               pltpu.VMEM((TILE_ROWS, d), _F32), pltpu.VMEM((TILE_ROWS, d), _F32),
               pltpu.VMEM((TILE_ROWS, d), _BF16), pltpu.VMEM((TILE_ROWS, d), _BF16)]
    if mixer == "conv":
        mgains, layer, w_in, b_in, dw, dw_b, ln_g, ln_b, w_out, b_out = mix_params
        j = (mix_index,)
        mix_specs = [_param(mgains, (layer,)), hbm, _param(b_in, j), _param(dw, j),
                     _param(dw_b, j), _param(ln_g, j), _param(ln_b, j), hbm, _param(b_out, j)]
        mix_args = [mgains, w_in, b_in, dw, dw_b, ln_g, ln_b, w_out, b_out]
        scratch += [pltpu.VMEM((d, 2 * d), _BF16), pltpu.VMEM((d, d), _BF16),
                    pltpu.VMEM((2, STAGE_ROWS_WIDE, 2 * d), _F32), pltpu.SemaphoreType.DMA((2,)),
                    pltpu.VMEM((CONV_HALO + CONV_SUB_ROWS, d), _F32),
                    pltpu.VMEM((CONV_SUB_ROWS, d), _F32)]
    elif mixer == "pool":
        mgains, layer, w_groups, scale = mix_params
        group = w_groups.shape[-1]
        mix_specs = [_param(mgains, (layer,)), hbm, _param(scale, (mix_index,))]
        mix_args = [mgains, w_groups, scale]
        scratch += [pltpu.VMEM(w_groups.shape[1:], _BF16),
                    pltpu.VMEM((2, group, group), _F32), pltpu.SemaphoreType.DMA((2,)),
                    pltpu.VMEM((POOL_HALO + TILE_ROWS, d), _F32),
                    pltpu.VMEM((POOL_HALO + TILE_ROWS, d), _F32),
                    pltpu.VMEM((TILE_ROWS, d), _F32)]
    else:
        mix_specs, mix_args = [], []

    return pl.pallas_call(
        functools.partial(_half_step_kernel, mixer=mixer, n_tiles=n_tiles,
                          tiles_per_seq=tiles_per_seq, index=index, mix_index=mix_index,
                          final_norm=final_norm),
        grid=(n_tiles + 1,),
        in_specs=[in_spec, _param(gains, index), _param(fgain, ()), hbm, hbm, hbm] + mix_specs,
        out_specs=out_spec,
        out_shape=jax.ShapeDtypeStruct((n, d), _F32),
        scratch_shapes=scratch,
        compiler_params=pltpu.CompilerParams(
            dimension_semantics=("arbitrary",), vmem_limit_bytes=VMEM_LIMIT_BYTES),
        name="half_step_" + mixer + ("_final" if final_norm else ""),
    )(x2, gains, fgain, wg, wu, wd, *mix_args)


def kernel(x, ffn_norm, ffn_w_gate, ffn_w_up, ffn_w_down, mix_norm, conv_w_in, conv_b_in,
           conv_dw, conv_dw_b, conv_ln_g, conv_ln_b, conv_w_out, conv_b_out, pool_w,
           pool_scale, final_norm):
    b, t, d = x.shape
    depth = ffn_norm.shape[0]
    assert t % TILE_ROWS == 0 and TILE_ROWS % CONV_SUB_ROWS == 0 and d % LANES == 0
    assert (b * t // TILE_ROWS) % 2 == 0
    ffn_gains, mix_gains, fgain = _rows(ffn_norm), _rows(mix_norm), _rows(final_norm)
    half = functools.partial(_half_step, tiles_per_seq=t // TILE_ROWS, gains=ffn_gains,
                             wg=ffn_w_gate, wu=ffn_w_up, wd=ffn_w_down, fgain=fgain)

    x2 = x.reshape(b * t, d)
    for i in range(depth):
        x2 = half(x2, index=(i, 0), final_norm=False)
        j = i // N_MIXERS
        last = i == depth - 1
        if i % N_MIXERS == 0:
            x2 = half(x2, index=(i, 1), final_norm=last, mixer="conv", mix_index=j,
                      mix_params=(mix_gains, i, conv_w_in, _rows(conv_b_in), conv_dw,
                                  _rows(conv_dw_b), _rows(conv_ln_g), _rows(conv_ln_b),
                                  conv_w_out, _rows(conv_b_out)))
        else:
            x2 = half(x2, index=(i, 1), final_norm=last, mixer="pool", mix_index=j,
                      mix_params=(mix_gains, i, pool_w, _rows(pool_scale)))
    return x2.reshape(b, t, d)
```

```python
import functools

import jax
import jax.numpy as jnp
from jax import lax
from jax.experimental import pallas as pl
from jax.experimental.pallas import tpu as pltpu

EPS = 1e-6
CONV_WIDTH = 31
POOL_WINDOWS = (2, 4, 8, 16)
N_MIXERS = 2

LANES = 128
SUBLANES = 8
MXU_DIM = 256
VMEM_LIMIT_BYTES = 60 * 1024 * 1024

FFN_ROWS = 512
FFN_CHUNK = 3 * MXU_DIM
WEIGHTS_IN_FLIGHT = 2
CONV_ROWS = 512
CONV_HALO = 32
POOL_HALO = 16
CONV_ROW_BLOCK = 64

_F32 = jnp.float32


def _rms_norm(x, gain):
    ms = jnp.mean(x * x, axis=-1, keepdims=True)
    return x * lax.rsqrt(ms + EPS) * gain


def _param(stacked, index):
    lead = len(index)
    tail = stacked.shape[lead:]
    at = tuple(index) + (0,) * len(tail)
    return pl.BlockSpec((None,) * lead + tail, lambda *_: at, pipeline_mode=pl.Buffered(1))


def _rows(stacked):
    return stacked.reshape(stacked.shape[:-1] + (1, stacked.shape[-1]))


def _chunks(total, size):
    return [(s, min(size, total - s)) for s in range(0, total, size)]


def _pool_tile(x, seq_pos, gain_ref, w_ref, scale_ref, hbuf_ref, sbuf_ref):
    rows, d = x.shape
    group = d // len(POOL_WINDOWS)

    @pl.when(seq_pos == 0)
    def _():
        hbuf_ref[0:POOL_HALO, :] = jnp.zeros((POOL_HALO, d), _F32)

    h = _rms_norm(x, gain_ref[...])
    hbuf_ref[POOL_HALO:POOL_HALO + rows, :] = h

    t_idx = seq_pos * rows + lax.broadcasted_iota(jnp.int32, (rows, 1), 0)
    outs = []
    for g, window in enumerate(POOL_WINDOWS):
        lanes = slice(g * group, (g + 1) * group)
        sbuf_ref[:, lanes] = hbuf_ref[:, lanes]
        span = 1
        while span < window:
            lo = POOL_HALO - (window - 2 * span)
            lo = max(lo - (lo % SUBLANES), span)
            cur = sbuf_ref[lo:POOL_HALO + rows, lanes]
            lag = sbuf_ref[lo - span:POOL_HALO + rows - span, lanes]
            sbuf_ref[lo:POOL_HALO + rows, lanes] = cur + lag
            span *= 2
        total = sbuf_ref[POOL_HALO:POOL_HALO + rows, lanes]
        cnt = jnp.minimum(t_idx + 1, window).astype(_F32)
        pooled = total / cnt - h[:, lanes]
        outs.append(jnp.dot(pooled, w_ref[g], preferred_element_type=_F32))
    hbuf_ref[0:POOL_HALO, :] = hbuf_ref[rows:rows + POOL_HALO, :]
    return x + jnp.concatenate(outs, axis=-1) * scale_ref[...]


def _ffn_kernel(*refs, index, pool, tiles_per_seq, final_norm):
    if pool:
        (x_ref, gain_ref, wg_hbm, wu_hbm, wd_hbm, fgain_ref, mgain_ref, pw_ref, pscale_ref,
         o_ref, wg_ref, wu_ref, wd_ref, sems, hbuf_ref, sbuf_ref) = refs
        x = _pool_tile(x_ref[...], pl.program_id(0) % tiles_per_seq, mgain_ref, pw_ref,
                       pscale_ref, hbuf_ref, sbuf_ref)
    else:
        (x_ref, gain_ref, wg_hbm, wu_hbm, wd_hbm, fgain_ref, o_ref,
         wg_ref, wu_ref, wd_ref, sems) = refs
        x = x_ref[...]
    chunks = _chunks(wg_ref.shape[1], FFN_CHUNK)

    def weight_copies(c):
        cols = pl.ds(*chunks[c])
        return [pltpu.make_async_copy(wg_hbm.at[index].at[:, cols], wg_ref.at[:, cols],
                                      sems.at[c, 0]),
                pltpu.make_async_copy(wu_hbm.at[index].at[:, cols], wu_ref.at[:, cols],
                                      sems.at[c, 1]),
                pltpu.make_async_copy(wd_hbm.at[index].at[cols, :], wd_ref.at[cols, :],
                                      sems.at[c, 2])]

    def half_step(streaming):
        if streaming:
            for c in range(min(WEIGHTS_IN_FLIGHT, len(chunks))):
                for copy in weight_copies(c):
                    copy.start()
        h = _rms_norm(x, gain_ref[...])
        acc = None
        for c, (start, width) in enumerate(chunks):
            if streaming:
                for copy in weight_copies(c):
                    copy.wait()
                if c + WEIGHTS_IN_FLIGHT < len(chunks):
                    for copy in weight_copies(c + WEIGHTS_IN_FLIGHT):
                        copy.start()
            g = jnp.dot(h, wg_ref[:, start:start + width], preferred_element_type=_F32)
            u = jnp.dot(h, wu_ref[:, start:start + width], preferred_element_type=_F32)
            a = g * jax.nn.sigmoid(g) * u
            d = jnp.dot(a, wd_ref[start:start + width, :], preferred_element_type=_F32)
            acc = d if acc is None else acc + d
        y = x + 0.5 * acc
        if final_norm:
            y = _rms_norm(y, fgain_ref[...])
        o_ref[...] = y

    @pl.when(pl.program_id(0) == 0)
    def _():
        half_step(streaming=True)

    @pl.when(pl.program_id(0) > 0)
    def _():
        half_step(streaming=False)


def _ffn(x2, tiles_per_seq, gains, wg, wu, wd, fgain, index, final_norm, pool_params=None):
    n, d = x2.shape
    f = wg.shape[-1]
    row_spec = pl.BlockSpec((FFN_ROWS, d), lambda i: (i, 0))
    hbm = pl.BlockSpec(memory_space=pl.ANY)
    in_specs = [row_spec, _param(gains, index), hbm, hbm, hbm, _param(fgain, ())]
    args = [x2, gains, wg, wu, wd, fgain]
    scratch = [pltpu.VMEM((d, f), _F32), pltpu.VMEM((d, f), _F32), pltpu.VMEM((f, d), _F32),
               pltpu.SemaphoreType.DMA((len(_chunks(f, FFN_CHUNK)), 3))]
    if pool_params is not None:
        mgains, layer, w_groups, scale, j = pool_params
        in_specs += [_param(mgains, (layer,)), _param(w_groups, (j,)), _param(scale, (j,))]
        args += [mgains, w_groups, scale]
        scratch += [pltpu.VMEM((POOL_HALO + FFN_ROWS, d), _F32),
                    pltpu.VMEM((POOL_HALO + FFN_ROWS, d), _F32)]
    return pl.pallas_call(
        functools.partial(_ffn_kernel, index=index, pool=pool_params is not None,
                          tiles_per_seq=tiles_per_seq, final_norm=final_norm),
        grid=(n // FFN_ROWS,),
        in_specs=in_specs,
        out_specs=row_spec,
        out_shape=jax.ShapeDtypeStruct((n, d), _F32),
        scratch_shapes=scratch,
        compiler_params=pltpu.CompilerParams(
            dimension_semantics=("arbitrary",), vmem_limit_bytes=VMEM_LIMIT_BYTES),
        name=("pool_ffn" if pool_params is not None else "ffn") + ("_final" if final_norm else ""),
    )(*args)


def _conv_kernel(x_ref, gain_ref, win_ref, bin_ref, dw_ref, dwb_ref, lng_ref, lnb_ref,
                 wout_ref, bout_ref, o_ref, vbuf_ref, y_ref, *, tiles_per_seq):
    rows, d = x_ref.shape

    @pl.when(pl.program_id(0) % tiles_per_seq == 0)
    def _():
        vbuf_ref[0:CONV_HALO, :] = jnp.zeros((CONV_HALO, d), _F32)

    x = x_ref[...]
    h = _rms_norm(x, gain_ref[...])
    u = jnp.dot(h, win_ref[...], preferred_element_type=_F32) + bin_ref[...]
    vbuf_ref[CONV_HALO:CONV_HALO + rows, :] = u[:, :d] * jax.nn.sigmoid(u[:, d:])

    first = CONV_HALO - (CONV_WIDTH - 1)
    window = CONV_ROW_BLOCK + CONV_HALO
    for r0 in range(0, rows, CONV_ROW_BLOCK):
        for l0 in range(0, d, LANES):
            src = vbuf_ref[r0:r0 + window, l0:l0 + LANES]
            acc = jnp.broadcast_to(dwb_ref[:, l0:l0 + LANES], (CONV_ROW_BLOCK, LANES))
            for phase in range(SUBLANES):
                shifted = src if phase == 0 else pltpu.roll(src, window - phase, axis=0)
                for k in range(CONV_WIDTH):
                    if (first + k) % SUBLANES != phase:
                        continue
                    q = first + k - phase
                    acc = acc + dw_ref[k:k + 1, l0:l0 + LANES] * shifted[q:q + CONV_ROW_BLOCK, :]
            y_ref[r0:r0 + CONV_ROW_BLOCK, l0:l0 + LANES] = acc

    vbuf_ref[0:CONV_HALO, :] = vbuf_ref[rows:rows + CONV_HALO, :]

    y = y_ref[...]
    mu = jnp.mean(y, axis=-1, keepdims=True)
    yc = y - mu
    var = jnp.mean(yc * yc, axis=-1, keepdims=True)
    z = yc * lax.rsqrt(var + EPS) * lng_ref[...] + lnb_ref[...]
    z = z * jax.nn.sigmoid(z)
    o_ref[...] = x + jnp.dot(z, wout_ref[...], preferred_element_type=_F32) + bout_ref[...]


def _conv_module(x2, tiles_per_seq, gains, layer, w_in, b_in, dw, dw_b, ln_g, ln_b, w_out,
                 b_out, j):
    n, d = x2.shape
    row_spec = pl.BlockSpec((CONV_ROWS, d), lambda i: (i, 0))
    params = (w_in, b_in, dw, dw_b, ln_g, ln_b, w_out, b_out)
    return pl.pallas_call(
        functools.partial(_conv_kernel, tiles_per_seq=tiles_per_seq),
        grid=(n // CONV_ROWS,),
        in_specs=[row_spec, _param(gains, (layer,))] + [_param(p, (j,)) for p in params],
        out_specs=row_spec,
        out_shape=jax.ShapeDtypeStruct((n, d), _F32),
        scratch_shapes=[pltpu.VMEM((CONV_HALO + CONV_ROWS, d), _F32),
                        pltpu.VMEM((CONV_ROWS, d), _F32)],
        compiler_params=pltpu.CompilerParams(
            dimension_semantics=("arbitrary",), vmem_limit_bytes=VMEM_LIMIT_BYTES),
        name="conv_module",
    )(x2, gains, *params)


def kernel(x, ffn_norm, ffn_w_gate, ffn_w_up, ffn_w_down, mix_norm, conv_w_in, conv_b_in,
           conv_dw, conv_dw_b, conv_ln_g, conv_ln_b, conv_w_out, conv_b_out, pool_w,
           pool_scale, final_norm):
    b, t, d = x.shape
    depth = ffn_norm.shape[0]
    assert t % FFN_ROWS == 0 and t % CONV_ROWS == 0 and d % LANES == 0
    ffn_gains, mix_gains, fgain = _rows(ffn_norm), _rows(mix_norm), _rows(final_norm)
    conv_params = (conv_w_in, _rows(conv_b_in), conv_dw, _rows(conv_dw_b), _rows(conv_ln_g),
                   _rows(conv_ln_b), conv_w_out, _rows(conv_b_out))
    ffn = functools.partial(_ffn, tiles_per_seq=t // FFN_ROWS, gains=ffn_gains, wg=ffn_w_gate,
                            wu=ffn_w_up, wd=ffn_w_down, fgain=fgain)

    x2 = x.reshape(b * t, d)
    for i in range(depth):
        x2 = ffn(x2, index=(i, 0), final_norm=False)
        j = i // N_MIXERS
        last = i == depth - 1
        if i % N_MIXERS == 0:
            x2 = _conv_module(x2, t // CONV_ROWS, mix_gains, i, *conv_params, j)
            x2 = ffn(x2, index=(i, 1), final_norm=last)
        else:
            x2 = ffn(x2, index=(i, 1), final_norm=last,
                     pool_params=(mix_gains, i, pool_w, _rows(pool_scale), j))
    return x2.reshape(b, t, d)
```

```python
import functools

import jax
import jax.numpy as jnp
from jax import lax
from jax.experimental import pallas as pl
from jax.experimental.pallas import tpu as pltpu

EPS = 1e-6
CONV_WIDTH = 31
POOL_WINDOWS = (2, 4, 8, 16)
N_MIXERS = 2

LANES = 128
SUBLANES = 8
MXU_DIM = 256
VMEM_LIMIT_BYTES = 60 * 1024 * 1024

FFN_ROWS = 512
FFN_CHUNK = 3 * MXU_DIM
WEIGHTS_IN_FLIGHT = 2
CONV_ROWS = 512
CONV_HALO = 32
POOL_HALO = 16
CONV_ROW_BLOCK = 128
CONV_FFN_ROWS = 512
CONV_TAP_ROWS = 64
STAGE_ROWS = 128

_F32 = jnp.float32
_BF16 = jnp.bfloat16


def _rms_norm(x, gain):
    ms = jnp.mean(x * x, axis=-1, keepdims=True)
    return x * lax.rsqrt(ms + EPS) * gain


def _param(stacked, index):
    lead = len(index)
    tail = stacked.shape[lead:]
    at = tuple(index) + (0,) * len(tail)
    return pl.BlockSpec((None,) * lead + tail, lambda *_: at, pipeline_mode=pl.Buffered(1))


def _rows(stacked):
    return stacked.reshape(stacked.shape[:-1] + (1, stacked.shape[-1]))


def _chunks(total, size):
    return [(s, min(size, total - s)) for s in range(0, total, size)]


def _pool_tile(x, seq_pos, gain_ref, w_ref, scale_ref, hbuf_ref, sbuf_ref):
    rows, d = x.shape
    group = d // len(POOL_WINDOWS)

    @pl.when(seq_pos == 0)
    def _():
        hbuf_ref[0:POOL_HALO, :] = jnp.zeros((POOL_HALO, d), _F32)

    h = _rms_norm(x, gain_ref[...])
    hbuf_ref[POOL_HALO:POOL_HALO + rows, :] = h

    t_idx = seq_pos * rows + lax.broadcasted_iota(jnp.int32, (rows, 1), 0)
    outs = []
    for g, window in enumerate(POOL_WINDOWS):
        lanes = slice(g * group, (g + 1) * group)
        sbuf_ref[:, lanes] = hbuf_ref[:, lanes]
        span = 1
        while span < window:
            lo = POOL_HALO - (window - 2 * span)
            lo = max(lo - (lo % SUBLANES), span)
            cur = sbuf_ref[lo:POOL_HALO + rows, lanes]
            lag = sbuf_ref[lo - span:POOL_HALO + rows - span, lanes]
            sbuf_ref[lo:POOL_HALO + rows, lanes] = cur + lag
            span *= 2
        total = sbuf_ref[POOL_HALO:POOL_HALO + rows, lanes]
        cnt = jnp.minimum(t_idx + 1, window).astype(_F32)
        pooled = total / cnt - h[:, lanes]
        outs.append(jnp.dot(pooled, w_ref[g], preferred_element_type=_F32))
    hbuf_ref[0:POOL_HALO, :] = hbuf_ref[rows:rows + POOL_HALO, :]
    return x + jnp.concatenate(outs, axis=-1) * scale_ref[...]


def _ffn_kernel(*refs, index, pool, tiles_per_seq, final_norm):
    if pool:
        (x_ref, gain_ref, wg_hbm, wu_hbm, wd_hbm, fgain_ref, mgain_ref, pw_ref, pscale_ref,
         o_ref, wg_ref, wu_ref, wd_ref, sems, hbuf_ref, sbuf_ref) = refs
        x = _pool_tile(x_ref[...], pl.program_id(0) % tiles_per_seq, mgain_ref, pw_ref,
                       pscale_ref, hbuf_ref, sbuf_ref)
    else:
        (x_ref, gain_ref, wg_hbm, wu_hbm, wd_hbm, fgain_ref, o_ref,
         wg_ref, wu_ref, wd_ref, sems) = refs
        x = x_ref[...]
    chunks = _chunks(wg_ref.shape[1], FFN_CHUNK)

    def weight_copies(c):
        cols = pl.ds(*chunks[c])
        return [pltpu.make_async_copy(wg_hbm.at[index].at[:, cols], wg_ref.at[:, cols],
                                      sems.at[c, 0]),
                pltpu.make_async_copy(wu_hbm.at[index].at[:, cols], wu_ref.at[:, cols],
                                      sems.at[c, 1]),
                pltpu.make_async_copy(wd_hbm.at[index].at[cols, :], wd_ref.at[cols, :],
                                      sems.at[c, 2])]

    def half_step(streaming):
        if streaming:
            for c in range(min(WEIGHTS_IN_FLIGHT, len(chunks))):
                for copy in weight_copies(c):
                    copy.start()
        h = _rms_norm(x, gain_ref[...])
        acc = None
        for c, (start, width) in enumerate(chunks):
            if streaming:
                for copy in weight_copies(c):
                    copy.wait()
                if c + WEIGHTS_IN_FLIGHT < len(chunks):
                    for copy in weight_copies(c + WEIGHTS_IN_FLIGHT):
                        copy.start()
            g = jnp.dot(h, wg_ref[:, start:start + width], preferred_element_type=_F32)
            u = jnp.dot(h, wu_ref[:, start:start + width], preferred_element_type=_F32)
            a = g * jax.nn.sigmoid(g) * u
            d = jnp.dot(a, wd_ref[start:start + width, :], preferred_element_type=_F32)
            acc = d if acc is None else acc + d
        y = x + 0.5 * acc
        if final_norm:
            y = _rms_norm(y, fgain_ref[...])
        o_ref[...] = y

    @pl.when(pl.program_id(0) == 0)
    def _():
        half_step(streaming=True)

    @pl.when(pl.program_id(0) > 0)
    def _():
        half_step(streaming=False)


def _ffn(x2, tiles_per_seq, gains, wg, wu, wd, fgain, index, final_norm, pool_params=None):
    n, d = x2.shape
    f = wg.shape[-1]
    row_spec = pl.BlockSpec((FFN_ROWS, d), lambda i: (i, 0))
    hbm = pl.BlockSpec(memory_space=pl.ANY)
    in_specs = [row_spec, _param(gains, index), hbm, hbm, hbm, _param(fgain, ())]
    args = [x2, gains, wg, wu, wd, fgain]
    scratch = [pltpu.VMEM((d, f), _F32), pltpu.VMEM((d, f), _F32), pltpu.VMEM((f, d), _F32),
               pltpu.SemaphoreType.DMA((len(_chunks(f, FFN_CHUNK)), 3))]
    if pool_params is not None:
        mgains, layer, w_groups, scale, j = pool_params
        in_specs += [_param(mgains, (layer,)), _param(w_groups, (j,)), _param(scale, (j,))]
        args += [mgains, w_groups, scale]
        scratch += [pltpu.VMEM((POOL_HALO + FFN_ROWS, d), _F32),
                    pltpu.VMEM((POOL_HALO + FFN_ROWS, d), _F32)]
    return pl.pallas_call(
        functools.partial(_ffn_kernel, index=index, pool=pool_params is not None,
                          tiles_per_seq=tiles_per_seq, final_norm=final_norm),
        grid=(n // FFN_ROWS,),
        in_specs=in_specs,
        out_specs=row_spec,
        out_shape=jax.ShapeDtypeStruct((n, d), _F32),
        scratch_shapes=scratch,
        compiler_params=pltpu.CompilerParams(
            dimension_semantics=("arbitrary",), vmem_limit_bytes=VMEM_LIMIT_BYTES),
        name=("pool_ffn" if pool_params is not None else "ffn") + ("_final" if final_norm else ""),
    )(*args)


def _conv_kernel(x_ref, gain_ref, win_ref, bin_ref, dw_ref, dwb_ref, lng_ref, lnb_ref,
                 wout_ref, bout_ref, o_ref, vbuf_ref, y_ref, *, tiles_per_seq):
    rows, d = x_ref.shape

    @pl.when(pl.program_id(0) % tiles_per_seq == 0)
    def _():
        vbuf_ref[0:CONV_HALO, :] = jnp.zeros((CONV_HALO, d), _F32)

    x = x_ref[...]
    h = _rms_norm(x, gain_ref[...])
    u = jnp.dot(h, win_ref[...], preferred_element_type=_F32) + bin_ref[...]
    vbuf_ref[CONV_HALO:CONV_HALO + rows, :] = u[:, :d] * jax.nn.sigmoid(u[:, d:])

    first = CONV_HALO - (CONV_WIDTH - 1)
    window = CONV_ROW_BLOCK + CONV_HALO
    for r0 in range(0, rows, CONV_ROW_BLOCK):
        for l0 in range(0, d, LANES):
            src = vbuf_ref[r0:r0 + window, l0:l0 + LANES]
            acc = jnp.broadcast_to(dwb_ref[:, l0:l0 + LANES], (CONV_ROW_BLOCK, LANES))
            for phase in range(SUBLANES):
                shifted = src if phase == 0 else pltpu.roll(src, window - phase, axis=0)
                for k in range(CONV_WIDTH):
                    if (first + k) % SUBLANES != phase:
                        continue
                    q = first + k - phase
                    acc = acc + dw_ref[k:k + 1, l0:l0 + LANES] * shifted[q:q + CONV_ROW_BLOCK, :]
            y_ref[r0:r0 + CONV_ROW_BLOCK, l0:l0 + LANES] = acc

    vbuf_ref[0:CONV_HALO, :] = vbuf_ref[rows:rows + CONV_HALO, :]

    y = y_ref[...]
    mu = jnp.mean(y, axis=-1, keepdims=True)
    yc = y - mu
    var = jnp.mean(yc * yc, axis=-1, keepdims=True)
    z = yc * lax.rsqrt(var + EPS) * lng_ref[...] + lnb_ref[...]
    z = z * jax.nn.sigmoid(z)
    o_ref[...] = x + jnp.dot(z, wout_ref[...], preferred_element_type=_F32) + bout_ref[...]


def _conv_module(x2, tiles_per_seq, gains, layer, w_in, b_in, dw, dw_b, ln_g, ln_b, w_out,
                 b_out, j):
    n, d = x2.shape
    row_spec = pl.BlockSpec((CONV_ROWS, d), lambda i: (i, 0))
    params = (w_in, b_in, dw, dw_b, ln_g, ln_b, w_out, b_out)
    return pl.pallas_call(
        functools.partial(_conv_kernel, tiles_per_seq=tiles_per_seq),
        grid=(n // CONV_ROWS,),
        in_specs=[row_spec, _param(gains, (layer,))] + [_param(p, (j,)) for p in params],
        out_specs=row_spec,
        out_shape=jax.ShapeDtypeStruct((n, d), _F32),
        scratch_shapes=[pltpu.VMEM((CONV_HALO + CONV_ROWS, d), _F32),
                        pltpu.VMEM((CONV_ROWS, d), _F32)],
        compiler_params=pltpu.CompilerParams(
            dimension_semantics=("arbitrary",), vmem_limit_bytes=VMEM_LIMIT_BYTES),
        name="conv_module",
    )(x2, gains, *params)


def _stage_weights(tasks):
    copies, used = [], {}
    for src, dst, stage, sems in tasks:
        rows = stage.shape[1]
        for c in range(src.shape[0] // rows):
            slot = used.get(id(stage), 0) % 2
            used[id(stage)] = used.get(id(stage), 0) + 1
            copies.append((src.at[pl.ds(c * rows, rows), :], dst.at[pl.ds(c * rows, rows), :],
                           stage.at[slot], sems.at[slot]))

    def dma(k):
        src, _, slot_ref, sem = copies[k]
        return pltpu.make_async_copy(src, slot_ref, sem)

    dma(0).start()
    for k, (_, dst, slot_ref, _) in enumerate(copies):
        if k + 1 < len(copies):
            dma(k + 1).start()
        dma(k).wait()
        dst[...] = slot_ref[...].astype(_BF16)


def _conv_ffn_kernel(x_ref, mgain_ref, bin_ref, dw_ref, dwb_ref, lng_ref, lnb_ref, bout_ref,
                     gain_ref, win_hbm, wout_hbm, wg_hbm, wu_hbm, wd_hbm, o_ref,
                     win_ref, wout_ref, wg_ref, wu_ref, wd_ref, stage_in, stage_col, stage_d,
                     sems_in, sems_col, sems_d, xbuf, hbuf, acc_ref, vbuf, ybuf, dwc,
                     *, n_tiles, tiles_per_seq, index, mix_index):
    rows, d = x_ref.shape
    n_lc = d // LANES
    n_blocks = (rows // CONV_TAP_ROWS) * n_lc
    n_chunks = wg_ref.shape[0]
    taps_per_chunk = -(-n_blocks // n_chunks)
    first = CONV_HALO - (CONV_WIDTH - 1)
    window = CONV_TAP_ROWS + CONV_HALO
    s = pl.program_id(0)
    slot = s % 2
    seq_pos = s % tiles_per_seq

    def conv_head():
        h = _rms_norm(x_ref[...], mgain_ref[...]).astype(_BF16)
        u = jnp.dot(h, win_ref[...], preferred_element_type=_F32) + bin_ref[...]
        v = u[:, :d] * jax.nn.sigmoid(u[:, d:])
        for lc in range(n_lc):
            halo = vbuf[lc, 0:CONV_HALO, :]
            vbuf[lc, 0:CONV_HALO, :] = jnp.where(
                jnp.full(halo.shape, seq_pos, jnp.int32) == 0, 0.0, halo)
            vbuf[lc, CONV_HALO:CONV_HALO + rows, :] = v[:, lc * LANES:(lc + 1) * LANES]

    def conv_taps(b):
        b = jnp.minimum(b, n_blocks - 1)
        lc = b % n_lc
        r0 = pl.multiple_of((b // n_lc) * CONV_TAP_ROWS, CONV_TAP_ROWS)
        src = vbuf[lc, pl.ds(r0, window), :]
        acc = jnp.broadcast_to(dwc[lc, CONV_WIDTH:CONV_WIDTH + 1, :], (CONV_TAP_ROWS, LANES))
        for phase in range(SUBLANES):
            shifted = src if phase == 0 else pltpu.roll(src, window - phase, axis=0)
            for k in range(CONV_WIDTH):
                if (first + k) % SUBLANES != phase:
                    continue
                q = first + k - phase
                acc = acc + dwc[lc, k:k + 1, :] * shifted[q:q + CONV_TAP_ROWS, :]
        ybuf[lc, pl.ds(r0, CONV_TAP_ROWS), :] = acc

    def conv_tail():
        for lc in range(n_lc):
            vbuf[lc, 0:CONV_HALO, :] = vbuf[lc, rows:rows + CONV_HALO, :]
        y = jnp.concatenate([ybuf[lc] for lc in range(n_lc)], axis=-1)
        mu = jnp.mean(y, axis=-1, keepdims=True)
        yc = y - mu
        var = jnp.mean(yc * yc, axis=-1, keepdims=True)
        z = yc * lax.rsqrt(var + EPS) * lng_ref[...] + lnb_ref[...]
        z = (z * jax.nn.sigmoid(z)).astype(_BF16)
        xn = x_ref[...] + jnp.dot(z, wout_ref[...], preferred_element_type=_F32) + bout_ref[...]
        xbuf[slot] = xn
        hbuf[slot] = _rms_norm(xn, gain_ref[...]).astype(_BF16)

    def swiglu_chunk(c):
        h = hbuf[1 - slot]
        g = jnp.dot(h, wg_ref[c], preferred_element_type=_F32)
        u = jnp.dot(h, wu_ref[c], preferred_element_type=_F32)
        a = (g * jax.nn.sigmoid(g) * u).astype(_BF16)
        width = wg_ref.shape[2]
        w_down = wd_ref[pl.ds(pl.multiple_of(c * width, width), width), :]
        acc_ref[...] += jnp.dot(a, w_down, preferred_element_type=_F32)

    def swiglu_finish():
        o_ref[...] = xbuf[1 - slot] + 0.5 * acc_ref[...]

    def run(conv, swiglu):
        if conv:
            conv_head()
        if swiglu:
            acc_ref[...] = jnp.zeros(acc_ref.shape, _F32)

        def body(c, carry):
            if swiglu:
                swiglu_chunk(c)
            if conv:
                for t in range(taps_per_chunk):
                    conv_taps(c * taps_per_chunk + t)
            return carry
        lax.fori_loop(0, n_chunks, body, 0)
        if conv:
            conv_tail()
        if swiglu:
            swiglu_finish()

    @pl.when(s == 0)
    def _():
        width = wg_ref.shape[2]
        tasks = [(win_hbm.at[mix_index], win_ref, stage_in, sems_in),
                 (wout_hbm.at[mix_index], wout_ref, stage_d, sems_d),
                 (wd_hbm.at[index], wd_ref, stage_d, sems_d)]
        for c in range(n_chunks):
            cols = pl.ds(c * width, width)
            tasks += [(wg_hbm.at[index].at[:, cols], wg_ref.at[c], stage_col, sems_col),
                      (wu_hbm.at[index].at[:, cols], wu_ref.at[c], stage_col, sems_col)]
        _stage_weights(tasks)
        for lc in range(n_lc):
            lanes = slice(lc * LANES, (lc + 1) * LANES)
            dwc[lc, 0:CONV_WIDTH, :] = dw_ref[:, lanes]
            dwc[lc, CONV_WIDTH:CONV_WIDTH + 1, :] = dwb_ref[:, lanes]
            vbuf[lc, 0:CONV_HALO, :] = jnp.zeros((CONV_HALO, LANES), _F32)
        run(conv=True, swiglu=False)

    @pl.when((s > 0) & (s < n_tiles))
    def _():
        run(conv=True, swiglu=True)

    @pl.when(s == n_tiles)
    def _():
        run(conv=False, swiglu=True)


def _conv_ffn(x2, tiles_per_seq, mgains, layer, w_in, b_in, dw, dw_b, ln_g, ln_b, w_out, b_out,
              j, gains, wg, wu, wd, index):
    n, d = x2.shape
    f = wg.shape[-1]
    rows = CONV_FFN_ROWS
    n_tiles = n // rows
    n_chunks = f // MXU_DIM
    hbm = pl.BlockSpec(memory_space=pl.ANY)
    in_spec = pl.BlockSpec((rows, d), lambda s: (jnp.minimum(s, n_tiles - 1), 0))
    out_spec = pl.BlockSpec((rows, d), lambda s: (jnp.maximum(s - 1, 0), 0))
    jj = (j,)
    return pl.pallas_call(
        functools.partial(_conv_ffn_kernel, n_tiles=n_tiles, tiles_per_seq=tiles_per_seq,
                          index=index, mix_index=j),
        grid=(n_tiles + 1,),
        in_specs=[in_spec, _param(mgains, (layer,)), _param(b_in, jj), _param(dw, jj),
                  _param(dw_b, jj), _param(ln_g, jj), _param(ln_b, jj), _param(b_out, jj),
                  _param(gains, index), hbm, hbm, hbm, hbm, hbm],
        out_specs=out_spec,
        out_shape=jax.ShapeDtypeStruct((n, d), _F32),
        scratch_shapes=[
            pltpu.VMEM((d, 2 * d), _BF16), pltpu.VMEM((d, d), _BF16),
            pltpu.VMEM((n_chunks, d, MXU_DIM), _BF16), pltpu.VMEM((n_chunks, d, MXU_DIM), _BF16),
            pltpu.VMEM((f, d), _BF16),
            pltpu.VMEM((2, STAGE_ROWS, 2 * d), _F32), pltpu.VMEM((2, d, MXU_DIM), _F32),
            pltpu.VMEM((2, STAGE_ROWS, d), _F32),
            pltpu.SemaphoreType.DMA((2,)), pltpu.SemaphoreType.DMA((2,)),
            pltpu.SemaphoreType.DMA((2,)),
            pltpu.VMEM((2, rows, d), _F32), pltpu.VMEM((2, rows, d), _BF16),
            pltpu.VMEM((rows, d), _F32),
            pltpu.VMEM((d // LANES, CONV_HALO + rows, LANES), _F32),
            pltpu.VMEM((d // LANES, rows, LANES), _F32),
            pltpu.VMEM((d // LANES, CONV_HALO, LANES), _F32)],
        compiler_params=pltpu.CompilerParams(
            dimension_semantics=("arbitrary",), vmem_limit_bytes=VMEM_LIMIT_BYTES),
        name="conv_ffn",
    )(x2, mgains, b_in, dw, dw_b, ln_g, ln_b, b_out, gains, w_in, w_out, wg, wu, wd)


def kernel(x, ffn_norm, ffn_w_gate, ffn_w_up, ffn_w_down, mix_norm, conv_w_in, conv_b_in,
           conv_dw, conv_dw_b, conv_ln_g, conv_ln_b, conv_w_out, conv_b_out, pool_w,
           pool_scale, final_norm):
    b, t, d = x.shape
    depth = ffn_norm.shape[0]
    assert t % FFN_ROWS == 0 and t % CONV_ROWS == 0 and d % LANES == 0
    ffn_gains, mix_gains, fgain = _rows(ffn_norm), _rows(mix_norm), _rows(final_norm)
    conv_params = (conv_w_in, _rows(conv_b_in), conv_dw, _rows(conv_dw_b), _rows(conv_ln_g),
                   _rows(conv_ln_b), conv_w_out, _rows(conv_b_out))
    ffn = functools.partial(_ffn, tiles_per_seq=t // FFN_ROWS, gains=ffn_gains, wg=ffn_w_gate,
                            wu=ffn_w_up, wd=ffn_w_down, fgain=fgain)

    x2 = x.reshape(b * t, d)
    for i in range(depth):
        x2 = ffn(x2, index=(i, 0), final_norm=False)
        j = i // N_MIXERS
        last = i == depth - 1
        if i % N_MIXERS == 0:
            x2 = _conv_ffn(x2, t // CONV_FFN_ROWS, mix_gains, i, *conv_params, j,
                           ffn_gains, ffn_w_gate, ffn_w_up, ffn_w_down, (i, 1))
        else:
            x2 = ffn(x2, index=(i, 1), final_norm=last,
                     pool_params=(mix_gains, i, pool_w, _rows(pool_scale), j))
    return x2.reshape(b, t, d)
```
